```python
import jax, jax.numpy as jnp
from jax import lax
import numpy as np

D_MODEL = 2048
BATCH = 16
SEQ = 2048
DEPTH = 4
DEC_BATCH = 1
DEC_SEQ = 8192
PAST_LEN = 128

HG_HEADS = 8
HG_KDIM = 128
HG_VDIM = 128
HG_KW = HG_HEADS * HG_KDIM
HG_WIDTH = HG_HEADS * HG_VDIM
HG_CHUNK = 64
HEAD_DIM = 128
ATT_HEADS = 8
ATT_KV_HEADS = 2
ATT_WIDTH = ATT_HEADS * HEAD_DIM
KV_WIDTH = ATT_KV_HEADS * HEAD_DIM
WINDOW = 128
ROPE_THETA = 10000.0
D_FF = ((-(-8 * D_MODEL // 3) + 255) // 256) * 256
DEEPNORM_ALPHA = (2 * DEPTH) ** 0.25
DEEPNORM_BETA = (8 * DEPTH) ** -0.25
LN_EPS = 1e-5
RMS_EPS = 1e-6
IN_SPLITS = (HG_KW, HG_KW, HG_KW, HG_WIDTH, HG_WIDTH, ATT_WIDTH, KV_WIDTH, KV_WIDTH, D_MODEL, D_MODEL)
IN_WIDTH = sum(IN_SPLITS)

kernel_name = 'hybrid_hgrn2_swa_deepnorm_encoder'


def _split_points():
    pts, acc = [], 0
    for w in IN_SPLITS[:-1]:
        acc += w
        pts.append(acc)
    return pts


def layer_norm(x, g, b):
    xf = x.astype(jnp.float32)
    mu = jnp.mean(xf, axis=-1, keepdims=True)
    var = jnp.mean(jnp.square(xf - mu), axis=-1, keepdims=True)
    y = (xf - mu) * lax.rsqrt(var + LN_EPS) * g.astype(jnp.float32) + b.astype(jnp.float32)
    return y.astype(x.dtype)


def rope_tables(L):
    inv = 1.0 / (ROPE_THETA ** (jnp.arange(0, HEAD_DIM, 2, dtype=jnp.float32) / HEAD_DIM))
    ang = jnp.arange(L, dtype=jnp.float32)[:, None] * inv[None, :]
    return jnp.cos(ang), jnp.sin(ang)


def apply_rope(x, cos, sin):
    x1, x2 = jnp.split(x.astype(jnp.float32), 2, axis=-1)
    c = cos[None, :, None, :]
    s = sin[None, :, None, :]
    return jnp.concatenate([x1 * c - x2 * s, x2 * c + x1 * s], axis=-1).astype(x.dtype)


def hgrn2_scan(q, k, v, log_f):
    B, L, H, dk = q.shape
    dv = v.shape[-1]
    C = HG_CHUNK
    N = L // C

    def chunks(t):
        return t.reshape(B, N, C, H, t.shape[-1]).transpose(1, 0, 3, 2, 4)

    causal = jnp.tril(jnp.ones((C, C), dtype=bool))[None, None, :, :, None]

    def step(S, inp):
        qc, kc, vc, gc = inp
        b = jnp.cumsum(gc, axis=2)
        b_end = b[:, :, -1:, :]
        o_inter = jnp.einsum('bhck,bhkv->bhcv', qc * jnp.exp(b), S)
        rel = jnp.where(causal, b[:, :, :, None, :] - b[:, :, None, :, :], -jnp.inf)
        scores = jnp.einsum('bhtk,bhsk,bhtsk->bhts', qc, kc, jnp.exp(rel))
        o_intra = jnp.einsum('bhts,bhsv->bhtv', scores, vc)
        S_new = jnp.exp(b_end[:, :, 0, :])[..., None] * S + jnp.einsum(
            'bhck,bhcv->bhkv', kc * jnp.exp(b_end - b), vc)
        return S_new, o_inter + o_intra

    S0 = jnp.zeros((B, H, dk, dv), jnp.float32)
    _, o = lax.scan(step, S0, (chunks(q), chunks(k), chunks(v), chunks(log_f)))
    return o.transpose(1, 0, 3, 2, 4).reshape(B, L, H, dv)


def hgrn2_branch(hq, hf_fwd, hf_bwd, hi, hg, lower_bound, norm_g):
    B, L, _ = hq.shape
    f32 = jnp.float32
    q = jax.nn.silu(hq.astype(f32)).reshape(B, L, HG_HEADS, HG_KDIM)
    v = hi.astype(f32).reshape(B, L, HG_HEADS, HG_VDIM)
    lb_fwd, lb_bwd = jnp.split(lower_bound, 2)

    def gates(z, lb):
        f = lb.reshape(HG_HEADS, HG_KDIM) + (1.0 - lb.reshape(HG_HEADS, HG_KDIM)) * jax.nn.sigmoid(
            z.astype(f32).reshape(B, L, HG_HEADS, HG_KDIM))
        return 1.0 - f, jnp.log(f)

    k_f, g_f = gates(hf_fwd, lb_fwd)
    k_b, g_b = gates(hf_bwd, lb_bwd)
    o_fwd = hgrn2_scan(q, k_f, v, g_f)
    flip = lambda t: jnp.flip(t, axis=1)
    o_bwd = flip(hgrn2_scan(flip(q), flip(k_b), flip(v), flip(g_b)))
    o = o_fwd + o_bwd
    o = o * lax.rsqrt(jnp.mean(jnp.square(o), axis=-1, keepdims=True) + RMS_EPS) * norm_g.astype(f32)
    o = o.reshape(B, L, HG_WIDTH) * jax.nn.silu(hg.astype(f32))
    return o.astype(hq.dtype)


def window_attention_branch(aq, ak, av, sink, cos, sin):
    B, L, _ = aq.shape
    W = WINDOW
    N = L // W
    G = ATT_HEADS // ATT_KV_HEADS
    q = apply_rope(aq.reshape(B, L, ATT_HEADS, HEAD_DIM), cos, sin)
    k = apply_rope(ak.reshape(B, L, ATT_KV_HEADS, HEAD_DIM), cos, sin)
    v = av.reshape(B, L, ATT_KV_HEADS, HEAD_DIM)
    qb = q.reshape(B, N, W, ATT_KV_HEADS, G, HEAD_DIM)

    def neighbours(t):
        tp = jnp.pad(t, ((0, 0), (W, W), (0, 0), (0, 0))).reshape(B, N + 2, W, ATT_KV_HEADS, HEAD_DIM)
        return jnp.concatenate([tp[:, :-2], tp[:, 1:-1], tp[:, 2:]], axis=2)

    kb = neighbours(k)
    vb = neighbours(v)
    s = jnp.einsum('bnqhgd,bnkhd->bnhgqk', qb, kb).astype(jnp.float32) * (HEAD_DIM ** -0.5)
    i = jnp.arange(W)[:, None]
    j = jnp.arange(3 * W)[None, :]
    n = jnp.arange(N)[:, None, None]
    kpos = (n - 1) * W + j[None]
    valid = (jnp.abs(j - W - i)[None] <= WINDOW) & (kpos >= 0) & (kpos < L)
    s = jnp.where(valid[None, :, None, None], s, -jnp.inf)
    sink_logit = jnp.broadcast_to(
        sink.astype(jnp.float32).reshape(1, 1, ATT_KV_HEADS, G, 1, 1), s.shape[:-1] + (1,))
    p = jax.nn.softmax(jnp.concatenate([s, sink_logit], axis=-1), axis=-1)[..., :-1]
    o = jnp.einsum('bnhgqk,bnkhd->bnqhgd', p.astype(vb.dtype), vb)
    return o.reshape(B, L, ATT_WIDTH)


def mixer(h, w_in, lower_bound, hg_norm_g, sink, w_branch_a, w_branch_b, w_out, cos, sin):
    proj = h @ w_in
    hq, hf_f, hf_b, hi, hg, aq, ak, av, gate_a, gate_b = jnp.split(proj, _split_points(), axis=-1)
    a = hgrn2_branch(hq, hf_f, hf_b, hi, hg, lower_bound, hg_norm_g)
    b = window_attention_branch(aq, ak, av, sink, cos, sin)
    merged = jax.nn.sigmoid(gate_a) * (a @ w_branch_a) + jax.nn.sigmoid(gate_b) * (b @ w_branch_b)
    return merged @ w_out


def swiglu(h, w_ffn_in, w_ffn_out):
    gate, up = jnp.split(h @ w_ffn_in, 2, axis=-1)
    return (jax.nn.silu(gate) * up) @ w_ffn_out


def encoder_trunk(x, ln_in_g, ln_in_b, w_in, lb_logits, hg_norm_g, attn_sink, w_branch_a, w_branch_b,
                  w_out, ln1_g, ln1_b, w_ffn_in, w_ffn_out, ln2_g, ln2_b):
    L = x.shape[1]
    cos, sin = rope_tables(L)
    p = jax.nn.softmax(lb_logits.astype(jnp.float32), axis=0)
    lower_bounds = jnp.cumsum(p, axis=0) - p[0:1]
    h = layer_norm(x, ln_in_g, ln_in_b)
    for l in range(DEPTH):
        mix = mixer(h, w_in[l], lower_bounds[l], hg_norm_g[l], attn_sink[l],
                    w_branch_a[l], w_branch_b[l], w_out[l], cos, sin)
        h = layer_norm(DEEPNORM_ALPHA * h + mix, ln1_g[l], ln1_b[l])
        h = layer_norm(DEEPNORM_ALPHA * h + swiglu(h, w_ffn_in[l], w_ffn_out[l]), ln2_g[l], ln2_b[l])
    return h


def setup_inputs(seed: int = 0) -> dict:
    key = jax.random.key(seed)
    ks = jax.random.split(key, 18)
    nrm = lambda k, shape, scale: jax.random.normal(k, shape, jnp.float32) * scale
    return {
        'x_prompt': nrm(ks[0], (BATCH, SEQ, D_MODEL), 1.0),
        'x_sample': nrm(ks[1], (DEC_BATCH, DEC_SEQ, D_MODEL), 1.0),
        'ln_in_g': 1.0 + nrm(ks[2], (D_MODEL,), 0.02),
        'ln_in_b': nrm(ks[3], (D_MODEL,), 0.02),
        'w_in': nrm(ks[4], (DEPTH, D_MODEL, IN_WIDTH), D_MODEL ** -0.5),
        'lb_logits': nrm(ks[5], (DEPTH, 2 * HG_KW), 0.1),
        'hg_norm_g': 1.0 + nrm(ks[6], (DEPTH, HG_VDIM), 0.02),
        'attn_sink': nrm(ks[7], (DEPTH, ATT_HEADS), 0.5),
        'w_branch_a': nrm(ks[8], (DEPTH, HG_WIDTH, D_MODEL), HG_WIDTH ** -0.5),
        'w_branch_b': nrm(ks[9], (DEPTH, ATT_WIDTH, D_MODEL), ATT_WIDTH ** -0.5),
        'w_out': nrm(ks[10], (DEPTH, D_MODEL, D_MODEL), DEEPNORM_BETA * D_MODEL ** -0.5),
        'ln1_g': 1.0 + nrm(ks[11], (DEPTH, D_MODEL), 0.02),
        'ln1_b': nrm(ks[12], (DEPTH, D_MODEL), 0.02),
        'w_ffn_in': nrm(ks[13], (DEPTH, D_MODEL, 2 * D_FF), D_MODEL ** -0.5),
        'w_ffn_out': nrm(ks[14], (DEPTH, D_FF, D_MODEL), DEEPNORM_BETA * D_FF ** -0.5),
        'ln2_g': 1.0 + nrm(ks[15], (DEPTH, D_MODEL), 0.02),
        'ln2_b': nrm(ks[16], (DEPTH, D_MODEL), 0.02),
    }


def reference(x_prompt, x_sample, ln_in_g, ln_in_b, w_in, lb_logits, hg_norm_g, attn_sink, w_branch_a,
              w_branch_b, w_out, ln1_g, ln1_b, w_ffn_in, w_ffn_out, ln2_g, ln2_b):
    y_prompt = encoder_trunk(x_prompt, ln_in_g, ln_in_b, w_in, lb_logits, hg_norm_g, attn_sink, w_branch_a,
                             w_branch_b, w_out, ln1_g, ln1_b, w_ffn_in, w_ffn_out, ln2_g, ln2_b)
    y_sample = encoder_trunk(x_sample, ln_in_g, ln_in_b, w_in, lb_logits, hg_norm_g, attn_sink, w_branch_a,
                             w_branch_b, w_out, ln1_g, ln1_b, w_ffn_in, w_ffn_out, ln2_g, ln2_b)
    return (y_prompt, y_sample)
```

```python
import functools
from typing import NamedTuple

import jax
import jax.numpy as jnp
from jax import lax
from jax.experimental import pallas as pl
from jax.experimental.pallas import tpu as pltpu

F32 = jnp.float32
BF16 = jnp.bfloat16

LANES = 128
HEAD = 128
HG_CHUNK = 64
HG_SUB = 16
HG_UNROLL = 4
EXP_CLAMP = 80.0
WINDOW = 128
ROPE_THETA = 10000.0
LN_EPS = 1e-5
RMS_EPS = 1e-6
VMEM_LIMIT = 56 * 1024 * 1024


class Dims(NamedTuple):
    d_model: int
    depth: int
    hg_heads: int
    att_heads: int
    kv_heads: int
    d_ff: int

    @property
    def hg_width(self):
        return self.hg_heads * HEAD

    @property
    def att_width(self):
        return self.att_heads * HEAD

    @property
    def kv_width(self):
        return self.kv_heads * HEAD

    @property
    def offsets(self):
        widths = (self.hg_width,) * 5 + (self.att_width, self.kv_width, self.kv_width,
                                        self.d_model, self.d_model)
        offs, acc = [], 0
        for w in widths:
            offs.append(acc)
            acc += w
        return tuple(offs), acc


def _params(sem):
    return pltpu.CompilerParams(dimension_semantics=sem, vmem_limit_bytes=VMEM_LIMIT)


def _sigmoid(x):
    return 1.0 / (1.0 + jnp.exp(-x))


def _layer_norm(y, g, b):
    mu = jnp.mean(y, axis=-1, keepdims=True)
    d = y - mu
    var = jnp.mean(d * d, axis=-1, keepdims=True)
    return d * lax.rsqrt(var + LN_EPS) * g + b


def _pick_tile(n, want):
    t = min(n, want)
    while n % t:
        t //= 2
    return t


def _ln_kernel(x_ref, g_ref, b_ref, o_ref):
    o_ref[...] = _layer_norm(x_ref[...], g_ref[...], b_ref[...])


def _input_layer_norm(x, g, b):
    t, d = x.shape
    tm = _pick_tile(t, 512)
    return pl.pallas_call(
        _ln_kernel,
        out_shape=jax.ShapeDtypeStruct((t, d), F32),
        grid=(t // tm,),
        in_specs=[pl.BlockSpec((tm, d), lambda i: (i, 0)),
                  pl.BlockSpec((1, d), lambda i: (0, 0)),
                  pl.BlockSpec((1, d), lambda i: (0, 0))],
        out_specs=pl.BlockSpec((tm, d), lambda i: (i, 0)),
        compiler_params=_params(("parallel",)),
        name="ln_in",
    )(x, g.reshape(1, d), b.reshape(1, d))


def _in_proj_kernel(h_ref, w_ref, o_ref, hb_ref):
    @pl.when(pl.program_id(1) == 0)
    def _():
        hb_ref[...] = h_ref[...].astype(BF16)

    o_ref[...] = jnp.dot(hb_ref[...], w_ref[...], preferred_element_type=F32).astype(o_ref.dtype)


def _in_proj(h, w, tm, tn):
    t, d = h.shape
    n = w.shape[1]
    return pl.pallas_call(
        _in_proj_kernel,
        out_shape=jax.ShapeDtypeStruct((t, n), BF16),
        grid=(t // tm, n // tn),
        in_specs=[pl.BlockSpec((tm, d), lambda i, j: (i, 0)),
                  pl.BlockSpec((d, tn), lambda i, j: (0, j))],
        out_specs=pl.BlockSpec((tm, tn), lambda i, j: (i, j)),
        scratch_shapes=[pltpu.VMEM((tm, d), BF16)],
        compiler_params=_params(("parallel", "arbitrary")),
        name="in_proj",
    )(h, w)


def _hgrn_chunk(rev, z, lb, q, v, st):
    c = HG_CHUNK
    f = lb + (1.0 - lb) * _sigmoid(z)
    k = 1.0 - f
    g = jnp.log(f)

    row = lax.broadcasted_iota(jnp.int32, (c, c), 0)
    col = lax.broadcasted_iota(jnp.int32, (c, c), 1)
    causal = (col >= row) if rev else (col <= row)
    tri = jnp.where(causal, 1.0, 0.0).astype(BF16)
    g1 = g.astype(BF16)
    r1 = g - g1.astype(F32)
    g2 = r1.astype(BF16)
    g3 = (r1 - g2.astype(F32)).astype(BF16)
    b = (jnp.dot(tri, g1, preferred_element_type=F32)
         + jnp.dot(tri, g2, preferred_element_type=F32)
         + jnp.dot(tri, g3, preferred_element_type=F32))

    b_end = b[0:1, :] if rev else b[c - 1:c, :]

    o = lax.dot_general((q * jnp.exp(b)).astype(BF16), st.astype(BF16),
                        (((1,), (1,)), ((), ())), preferred_element_type=F32)

    nsub = c // HG_SUB
    rows = []
    for i in range(nsub):
        lo, hi = i * HG_SUB, (i + 1) * HG_SUB
        if rev:
            ref = b[hi:hi + 1, :] if i < nsub - 1 else jnp.zeros_like(b_end)
        else:
            ref = b[lo - 1:lo, :] if i > 0 else jnp.zeros_like(b_end)
        qi = (q[lo:hi, :] * jnp.exp(b[lo:hi, :] - ref)).astype(BF16)
        ki = (k * jnp.exp(jnp.minimum(ref - b, EXP_CLAMP))).astype(BF16)
        s = lax.dot_general(qi, ki, (((1,), (1,)), ((), ())), preferred_element_type=F32)
        rows.append(jnp.where(causal[lo:hi, :], s, 0.0))
    a = jnp.concatenate(rows, axis=0).astype(BF16)
    o = o + jnp.dot(a, v, preferred_element_type=F32)

    kd = (k * jnp.exp(b_end - b)).astype(BF16)
    st_new = st * jnp.exp(b_end) + lax.dot_general(v, kd, (((0,), (0,)), ((), ())),
                                                   preferred_element_type=F32)
    return o, st_new


def _hgrn_kernel(q_ref, zf_ref, zb_ref, v_ref, gate_ref, lbf_ref, lbb_ref, ng_ref, o_ref, acc_ref,
                 *, seq_len):
    c = HG_CHUNK
    span = c * HG_UNROLL
    steps = seq_len // span

    def chunk_inputs(z_ref, r0):
        z = z_ref[pl.ds(r0, c), :].astype(F32)
        qz = q_ref[pl.ds(r0, c), :].astype(F32)
        return z, qz * _sigmoid(qz), v_ref[pl.ds(r0, c), :]

    def fwd_step(i, st):
        base = i * span
        for u in range(HG_UNROLL):
            r0 = pl.multiple_of(base + u * c, c)
            z, q, v = chunk_inputs(zf_ref, r0)
            o, st = _hgrn_chunk(False, z, lbf_ref[...], q, v, st)
            acc_ref[pl.ds(r0, c), :] = o
        return st

    def bwd_step(i, st):
        base = (steps - 1 - i) * span
        for u in reversed(range(HG_UNROLL)):
            r0 = pl.multiple_of(base + u * c, c)
            z, q, v = chunk_inputs(zb_ref, r0)
            o, st = _hgrn_chunk(True, z, lbb_ref[...], q, v, st)
            o = o + acc_ref[pl.ds(r0, c), :]
            o = o * lax.rsqrt(jnp.mean(o * o, axis=-1, keepdims=True) + RMS_EPS) * ng_ref[...]
            gz = gate_ref[pl.ds(r0, c), :].astype(F32)
            o_ref[pl.ds(r0, c), :] = (o * (gz * _sigmoid(gz))).astype(o_ref.dtype)
        return st

    st0 = jnp.zeros((HEAD, HEAD), F32)
    lax.fori_loop(0, steps, fwd_step, st0)
    lax.fori_loop(0, steps, bwd_step, st0)


def _hgrn_call(proj, lb, norm_g, dims, seq_len, nseq, row_block_off):
    offs, _ = dims.offsets
    nh = dims.hg_heads
    cb = [o // HEAD for o in offs[:5]]

    def sec(k):
        return pl.BlockSpec((seq_len, HEAD), lambda s, h: (s + row_block_off, cb[k] + h))

    return pl.pallas_call(
        functools.partial(_hgrn_kernel, seq_len=seq_len),
        out_shape=jax.ShapeDtypeStruct((nseq * seq_len, dims.hg_width), BF16),
        grid=(nseq, nh),
        in_specs=[sec(0), sec(1), sec(2), sec(3), sec(4),
                  pl.BlockSpec((1, HEAD), lambda s, h: (0, h)),
                  pl.BlockSpec((1, HEAD), lambda s, h: (0, nh + h)),
                  pl.BlockSpec((1, HEAD), lambda s, h: (0, 0))],
        out_specs=pl.BlockSpec((seq_len, HEAD), lambda s, h: (s, h)),
        scratch_shapes=[pltpu.VMEM((seq_len, HEAD), F32)],
        compiler_params=_params(("parallel", "arbitrary")),
        name=f"hgrn2_{seq_len}",
    )(proj, proj, proj, proj, proj, lb, lb, norm_g)


def _rope(x, cos, sin):
    return x * cos + pltpu.roll(x, HEAD // 2, axis=1) * sin


def _attn_kernel(sink_ref, q_ref, k_ref, kp_ref, kn_ref, v_ref, vp_ref, vn_ref,
                 cos_ref, sin_ref, cosp_ref, sinp_ref, cosn_ref, sinn_ref,
                 o_ref, kr_ref, vr_ref, *, group, blocks_a, nblk_a, seq_blocks_b):
    w = WINDOW
    rows = q_ref.shape[0]
    nq = rows // w
    blk = pl.program_id(0)
    kvh = pl.program_id(1)
    in_b = blk >= nblk_a
    idx_b = (blk - nblk_a) % seq_blocks_b
    has_prev = jnp.logical_and(in_b, idx_b > 0)
    has_next = jnp.logical_and(in_b, idx_b < seq_blocks_b - 1)
    del blocks_a

    kr_ref[0:w, :] = _rope(kp_ref[...].astype(F32), cosp_ref[...], sinp_ref[...]).astype(BF16)
    kr_ref[w:w + rows, :] = _rope(k_ref[...].astype(F32), cos_ref[...], sin_ref[...]).astype(BF16)
    kr_ref[w + rows:, :] = _rope(kn_ref[...].astype(F32), cosn_ref[...], sinn_ref[...]).astype(BF16)
    vr_ref[0:w, :] = vp_ref[...]
    vr_ref[w:w + rows, :] = v_ref[...]
    vr_ref[w + rows:, :] = vn_ref[...]

    qi = lax.broadcasted_iota(jnp.int32, (w, 3 * w), 0)
    kj = lax.broadcasted_iota(jnp.int32, (w, 3 * w), 1)
    band = jnp.abs(kj - w - qi) <= w
    scale = HEAD ** -0.5

    def body(n, carry):
        r0 = pl.multiple_of(n * w, w)
        cos = cos_ref[pl.ds(r0, w), :]
        sin = sin_ref[pl.ds(r0, w), :]
        qs = [_rope(q_ref[pl.ds(r0, w), g * HEAD:(g + 1) * HEAD].astype(F32), cos, sin).astype(BF16)
              for g in range(group)]
        q = jnp.concatenate(qs, axis=0)
        keys = kr_ref[pl.ds(r0, 3 * w), :]
        vals = vr_ref[pl.ds(r0, 3 * w), :]
        s = lax.dot_general(q, keys, (((1,), (1,)), ((), ())), preferred_element_type=F32) * scale
        valid = band
        valid = jnp.logical_and(valid, jnp.logical_or(kj >= w, jnp.logical_or(n > 0, has_prev)))
        valid = jnp.logical_and(valid, jnp.logical_or(kj < 2 * w,
                                                      jnp.logical_or(n < nq - 1, has_next)))
        ps, dens = [], []
        for g in range(group):
            sg = jnp.where(valid, s[g * w:(g + 1) * w, :], -1e30)
            sink = sink_ref[kvh * group + g]
            m = jnp.maximum(jnp.max(sg, axis=-1, keepdims=True), sink)
            p = jnp.exp(sg - m)
            dens.append(jnp.sum(p, axis=-1, keepdims=True) + jnp.exp(sink - m))
            ps.append(p.astype(BF16))
        o = jnp.dot(jnp.concatenate(ps, axis=0), vals, preferred_element_type=F32)
        for g in range(group):
            og = o[g * w:(g + 1) * w, :] / dens[g]
            o_ref[pl.ds(r0, w), g * HEAD:(g + 1) * HEAD] = og.astype(o_ref.dtype)
        return carry

    lax.fori_loop(0, nq, body, 0)


def _attention(proj, sink, cos, sin, dims, rows, nblk_a, seq_blocks_b):
    t = proj.shape[0]
    offs, _ = dims.offsets
    group = dims.att_heads // dims.kv_heads
    w = WINDOW
    nblk = t // rows
    per = rows // w
    last_w = t // w - 1
    qcb = offs[5] // (group * HEAD)
    kcb = offs[6] // HEAD
    vcb = offs[7] // HEAD
    pos_last_w = cos.shape[0] // w - 1

    def pos_blk(b):
        return jnp.where(b < nblk_a, 0, (b - nblk_a) % seq_blocks_b)

    def prev_w(b):
        return jnp.maximum(b * per - 1, 0)

    def next_w(b):
        return jnp.minimum(b * per + per, last_w)

    def pos_prev_w(b):
        return jnp.maximum(pos_blk(b) * per - 1, 0)

    def pos_next_w(b):
        return jnp.minimum(pos_blk(b) * per + per, pos_last_w)

    in_specs = [
        pl.BlockSpec(memory_space=pltpu.SMEM),
        pl.BlockSpec((rows, group * HEAD), lambda b, h: (b, qcb + h)),
        pl.BlockSpec((rows, HEAD), lambda b, h: (b, kcb + h)),
        pl.BlockSpec((w, HEAD), lambda b, h: (prev_w(b), kcb + h)),
        pl.BlockSpec((w, HEAD), lambda b, h: (next_w(b), kcb + h)),
        pl.BlockSpec((rows, HEAD), lambda b, h: (b, vcb + h)),
        pl.BlockSpec((w, HEAD), lambda b, h: (prev_w(b), vcb + h)),
        pl.BlockSpec((w, HEAD), lambda b, h: (next_w(b), vcb + h)),
        pl.BlockSpec((rows, HEAD), lambda b, h: (pos_blk(b), 0)),
        pl.BlockSpec((rows, HEAD), lambda b, h: (pos_blk(b), 0)),
        pl.BlockSpec((w, HEAD), lambda b, h: (pos_prev_w(b), 0)),
        pl.BlockSpec((w, HEAD), lambda b, h: (pos_prev_w(b), 0)),
        pl.BlockSpec((w, HEAD), lambda b, h: (pos_next_w(b), 0)),
        pl.BlockSpec((w, HEAD), lambda b, h: (pos_next_w(b), 0)),
    ]
    return pl.pallas_call(
        functools.partial(_attn_kernel, group=group, blocks_a=None, nblk_a=nblk_a,
                          seq_blocks_b=seq_blocks_b),
        out_shape=jax.ShapeDtypeStruct((t, dims.att_width), BF16),
        grid=(nblk, dims.kv_heads),
        in_specs=in_specs,
        out_specs=pl.BlockSpec((rows, group * HEAD), lambda b, h: (b, h)),
        scratch_shapes=[pltpu.VMEM((rows + 2 * w, HEAD), BF16),
                        pltpu.VMEM((rows + 2 * w, HEAD), BF16)],
        compiler_params=_params(("parallel", "arbitrary")),
        name="window_attn",
    )(sink, proj, proj, proj, proj, proj, proj, proj, cos, sin, cos, sin, cos, sin)


def _merge_kernel(a_ref, b_ref, ga_ref, gb_ref, wa_ref, wb_ref, o_ref):
    pa = jnp.dot(a_ref[...], wa_ref[...], preferred_element_type=F32)
    pb = jnp.dot(b_ref[...], wb_ref[...], preferred_element_type=F32)
    m = _sigmoid(ga_ref[...].astype(F32)) * pa + _sigmoid(gb_ref[...].astype(F32)) * pb
    o_ref[...] = m.astype(o_ref.dtype)


def _merge(a, b, proj, wa, wb, dims, tm, tn):
    t = a.shape[0]
    d = dims.d_model
    offs, _ = dims.offsets
    gacb, gbcb = offs[8] // tn, offs[9] // tn
    return pl.pallas_call(
        _merge_kernel,
        out_shape=jax.ShapeDtypeStruct((t, d), BF16),
        grid=(t // tm, d // tn),
        in_specs=[pl.BlockSpec((tm, a.shape[1]), lambda i, j: (i, 0)),
                  pl.BlockSpec((tm, b.shape[1]), lambda i, j: (i, 0)),
                  pl.BlockSpec((tm, tn), lambda i, j: (i, gacb + j)),
                  pl.BlockSpec((tm, tn), lambda i, j: (i, gbcb + j)),
                  pl.BlockSpec((wa.shape[0], tn), lambda i, j: (0, j)),
                  pl.BlockSpec((wb.shape[0], tn), lambda i, j: (0, j))],
        out_specs=pl.BlockSpec((tm, tn), lambda i, j: (i, j)),
        compiler_params=_params(("parallel", "arbitrary")),
        name="merge",
    )(a, b, proj, proj, wa, wb)


def _out_proj_kernel(m_ref, w_ref, h_ref, g_ref, b_ref, o_ref, *, alpha):
    y = alpha * h_ref[...] + jnp.dot(m_ref[...], w_ref[...], preferred_element_type=F32)
    o_ref[...] = _layer_norm(y, g_ref[...], b_ref[...])


def _out_proj_ln(m, w, h, g, b, alpha, tm):
    t, d = h.shape
    return pl.pallas_call(
        functools.partial(_out_proj_kernel, alpha=alpha),
        out_shape=jax.ShapeDtypeStruct((t, d), F32),
        grid=(t // tm,),
        in_specs=[pl.BlockSpec((tm, d), lambda i: (i, 0)),
                  pl.BlockSpec((d, d), lambda i: (0, 0)),
                  pl.BlockSpec((tm, d), lambda i: (i, 0)),
                  pl.BlockSpec((1, d), lambda i: (0, 0)),
                  pl.BlockSpec((1, d), lambda i: (0, 0))],
        out_specs=pl.BlockSpec((tm, d), lambda i: (i, 0)),
        compiler_params=_params(("parallel",)),
        name="out_proj_ln",
    )(m, w, h, g, b)


def _ffn_kernel(h_ref, wg_ref, wu_ref, wo_ref, g_ref, b_ref, o_ref, hb_ref, *, alpha):
    f = pl.program_id(1)

    @pl.when(f == 0)
    def _():
        hb_ref[...] = h_ref[...].astype(BF16)
        o_ref[...] = alpha * h_ref[...]

    x = hb_ref[...]
    gate = jnp.dot(x, wg_ref[...], preferred_element_type=F32)
    up = jnp.dot(x, wu_ref[...], preferred_element_type=F32)
    act = (gate * _sigmoid(gate) * up).astype(BF16)
    o_ref[...] += jnp.dot(act, wo_ref[...], preferred_element_type=F32)

    @pl.when(f == pl.num_programs(1) - 1)
    def _():
        o_ref[...] = _layer_norm(o_ref[...], g_ref[...], b_ref[...])


def _ffn_ln(h, w_in, w_out, g, b, alpha, tm, tf):
    t, d = h.shape
    d_ff = w_out.shape[0]
    nf = d_ff // tf
    return pl.pallas_call(
        functools.partial(_ffn_kernel, alpha=alpha),
        out_shape=jax.ShapeDtypeStruct((t, d), F32),
        grid=(t // tm, nf),
        in_specs=[pl.BlockSpec((tm, d), lambda i, f: (i, 0)),
                  pl.BlockSpec((d, tf), lambda i, f: (0, f)),
                  pl.BlockSpec((d, tf), lambda i, f: (0, nf + f)),
                  pl.BlockSpec((tf, d), lambda i, f: (f, 0)),
                  pl.BlockSpec((1, d), lambda i, f: (0, 0)),
                  pl.BlockSpec((1, d), lambda i, f: (0, 0))],
        out_specs=pl.BlockSpec((tm, d), lambda i, f: (i, 0)),
        scratch_shapes=[pltpu.VMEM((tm, d), BF16)],
        compiler_params=_params(("parallel", "arbitrary")),
        name="ffn_ln",
    )(h, w_in, w_in, w_out, g, b)


def _rope_tables(length):
    inv = 1.0 / (ROPE_THETA ** (jnp.arange(0, HEAD, 2, dtype=F32) / HEAD))
    ang = jnp.arange(length, dtype=F32)[:, None] * inv[None, :]
    c, s = jnp.cos(ang), jnp.sin(ang)
    return jnp.concatenate([c, c], axis=-1), jnp.concatenate([-s, s], axis=-1)


def _trunk(x_a, x_b, ln_in_g, ln_in_b, w_in, lb_logits, hg_norm_g, attn_sink, w_branch_a, w_branch_b,
           w_out, ln1_g, ln1_b, w_ffn_in, w_ffn_out, ln2_g, ln2_b, dims):
    d = dims.d_model
    ba, la, _ = x_a.shape
    bb, lb_len, _ = x_b.shape
    assert bb == 1 and lb_len % la == 0
    ta, tb = ba * la, lb_len
    assert ta % lb_len == 0
    t = ta + tb
    alpha = (2 * dims.depth) ** 0.25

    x = jnp.concatenate([x_a.reshape(ta, d), x_b.reshape(tb, d)], axis=0)
    cos, sin = _rope_tables(max(la, lb_len))
    p = jax.nn.softmax(lb_logits.astype(F32), axis=0)
    lower = jnp.cumsum(p, axis=0) - p[0:1]

    w_in_b = w_in.astype(BF16)
    wa_b = w_branch_a.astype(BF16)
    wb_b = w_branch_b.astype(BF16)
    wo_b = w_out.astype(BF16)
    wfi_b = w_ffn_in.astype(BF16)
    wfo_b = w_ffn_out.astype(BF16)

    _, in_width = dims.offsets
    tm_big = _pick_tile(t, 1024)
    tm_mid = _pick_tile(t, 512)
    tn_in = _pick_tile(in_width, 768) if in_width % 768 == 0 else _pick_tile(in_width, 512)
    tn_merge = _pick_tile(d, 512)
    tf = _pick_tile(dims.d_ff, 512)

    h = _input_layer_norm(x, ln_in_g, ln_in_b)
    for l in range(dims.depth):
        proj = _in_proj(h, w_in_b[l], tm_big, tn_in)
        lbl = lower[l].reshape(1, -1)
        ng = hg_norm_g[l].reshape(1, HEAD)
        a = jnp.concatenate([
            _hgrn_call(proj, lbl, ng, dims, la, ba, 0),
            _hgrn_call(proj, lbl, ng, dims, lb_len, 1, ta // lb_len)], axis=0)
        b = _attention(proj, attn_sink[l].astype(F32), cos, sin, dims, la, ba, lb_len // la)
        m = _merge(a, b, proj, wa_b[l], wb_b[l], dims, tm_big, tn_merge)
        h = _out_proj_ln(m, wo_b[l], h, ln1_g[l].reshape(1, d), ln1_b[l].reshape(1, d), alpha, tm_mid)
        h = _ffn_ln(h, wfi_b[l], wfo_b[l], ln2_g[l].reshape(1, d), ln2_b[l].reshape(1, d), alpha,
                    tm_mid, tf)
    return h[:ta].reshape(ba, la, d), h[ta:].reshape(1, lb_len, d)


def kernel(x_prompt, x_sample, ln_in_g, ln_in_b, w_in, lb_logits, hg_norm_g, attn_sink, w_branch_a,
           w_branch_b, w_out, ln1_g, ln1_b, w_ffn_in, w_ffn_out, ln2_g, ln2_b):
    d = x_prompt.shape[-1]
    dims = Dims(d_model=d, depth=w_in.shape[0], hg_heads=w_branch_a.shape[1] // HEAD,
                att_heads=attn_sink.shape[1],
                kv_heads=(w_in.shape[2] - 5 * w_branch_a.shape[1] - w_branch_b.shape[1] - 2 * d)
                // (2 * HEAD),
                d_ff=w_ffn_out.shape[1])
    return _trunk(x_prompt, x_sample, ln_in_g, ln_in_b, w_in, lb_logits, hg_norm_g, attn_sink,
                  w_branch_a, w_branch_b, w_out, ln1_g, ln1_b, w_ffn_in, w_ffn_out, ln2_g, ln2_b, dims)
```
